```python
import math
import jax, jax.numpy as jnp
from jax import lax
import numpy as np

D_MODEL = 2048
BATCH = 4
SEQ = 4096
DEPTH = 4

N_MIXERS = 2
N_POOL_LAYERS = (DEPTH + 1) // 2
N_MLA_LAYERS = DEPTH // 2
N_SUBLAYERS = 3
N_MOD = 3
D_FF = 5632
POOL_WINDOWS = (2, 4, 8, 16)
N_POOL_GROUPS = 4
POOL_GROUP_DIM = D_MODEL // N_POOL_GROUPS
MLA_HEADS = 16
Q_LORA_RANK = 512
KV_LORA_RANK = 512
QK_NOPE_DIM = 128
QK_ROPE_DIM = 64
V_HEAD_DIM = 128
QK_HEAD_DIM = QK_NOPE_DIM + QK_ROPE_DIM
ROPE_THETA = 10000.0
Q_BLOCK = 128
NORM_EPS = 1e-6
ADA_INIT = 0.1

kernel_name = "hybrid_pool_mla_macaron_adaln"


def _rms_norm(x, g):
    xf = x.astype(jnp.float32)
    y = xf * lax.rsqrt(jnp.mean(xf * xf, axis=-1, keepdims=True) + NORM_EPS)
    return (y * g.astype(jnp.float32)).astype(x.dtype)


def _swiglu(h, w_gate, w_up, w_down):
    return (jax.nn.silu(h @ w_gate) * (h @ w_up)) @ w_down


def _rope(x, cos, sin):
    x1, x2 = jnp.split(x, 2, axis=-1)
    return jnp.concatenate([x1 * cos - x2 * sin, x2 * cos + x1 * sin], axis=-1)


def _sublayer(x, mod_s, g_pre, g_post, fn, weight):
    shift = mod_s[:, 0, None, :]
    scale = mod_s[:, 1, None, :]
    gate = mod_s[:, 2, None, :]
    h = _rms_norm(x, g_pre) * (1.0 + scale) + shift
    y = _rms_norm(fn(h), g_post)
    return x + weight * (1.0 + gate) * y


def _pool_mixer(h, w_group, b_group, ch_scale):
    s = h.shape[1]
    hf = h.astype(jnp.float32)
    t = jnp.arange(s)
    outs = []
    for g, w in enumerate(POOL_WINDOWS):
        hg = hf[..., g * POOL_GROUP_DIM:(g + 1) * POOL_GROUP_DIM]
        cs = jnp.cumsum(hg, axis=1)
        cs_lag = jnp.pad(cs, ((0, 0), (w, 0), (0, 0)))[:, :s]
        count = jnp.minimum(t + 1, w).astype(jnp.float32)
        pooled = (cs - cs_lag) / count[None, :, None] - hg
        outs.append(pooled.astype(h.dtype) @ w_group[g] + b_group[g])
    return jnp.concatenate(outs, axis=-1) * ch_scale


def _mla(h, positions, w_dq, q_norm, w_uq, w_dkv, kv_norm, w_ukv, w_o):
    b, s, _ = h.shape
    inv_freq = ROPE_THETA ** (-jnp.arange(0, QK_ROPE_DIM, 2, dtype=jnp.float32) / QK_ROPE_DIM)
    ang = positions.astype(jnp.float32)[..., None] * inv_freq
    cos = jnp.cos(ang).astype(h.dtype)
    sin = jnp.sin(ang).astype(h.dtype)
    c_q = _rms_norm(h @ w_dq, q_norm)
    q = (c_q @ w_uq).reshape(b, s, MLA_HEADS, QK_HEAD_DIM)
    q_nope = q[..., :QK_NOPE_DIM]
    q_rope = _rope(q[..., QK_NOPE_DIM:], cos[:, :, None, :], sin[:, :, None, :])
    ckv = h @ w_dkv
    c_kv = _rms_norm(ckv[..., :KV_LORA_RANK], kv_norm)
    k_rope = _rope(ckv[..., KV_LORA_RANK:], cos, sin)
    kv = (c_kv @ w_ukv).reshape(b, s, MLA_HEADS, QK_NOPE_DIM + V_HEAD_DIM)
    k_nope = kv[..., :QK_NOPE_DIM]
    v = kv[..., QK_NOPE_DIM:]
    sm_scale = QK_HEAD_DIM ** -0.5
    neg = jnp.finfo(jnp.float32).min
    outs = []
    for i in range(s // Q_BLOCK):
        qs, qe = i * Q_BLOCK, (i + 1) * Q_BLOCK
        sc = jnp.einsum('bqhd,bkhd->bhqk', q_nope[:, qs:qe], k_nope[:, :qe],
                        preferred_element_type=jnp.float32)
        sc = sc + jnp.einsum('bqhr,bkr->bhqk', q_rope[:, qs:qe], k_rope[:, :qe],
                             preferred_element_type=jnp.float32)
        causal = jnp.arange(qs, qe)[:, None] >= jnp.arange(qe)[None, :]
        sc = jnp.where(causal[None, None], sc * sm_scale, neg)
        p = jax.nn.softmax(sc, axis=-1)
        outs.append(jnp.einsum('bhqk,bkhd->bqhd', p.astype(v.dtype), v[:, :qe]))
    o = jnp.concatenate(outs, axis=1).reshape(b, s, MLA_HEADS * V_HEAD_DIM)
    return o @ w_o


def setup_inputs(seed: int = 0) -> dict:
    key = jax.random.key(seed)
    ks = jax.random.split(key, 24)
    f32 = jnp.float32

    def nrm(k, shape, fan_in, mult=1.0):
        return jax.random.normal(k, shape, f32) * (mult * fan_in ** -0.5)

    def gain(k, shape, s=0.05):
        return 1.0 + s * jax.random.normal(k, shape, f32)

    def small(k, shape):
        return 0.01 * jax.random.normal(k, shape, f32)

    n_mod = N_SUBLAYERS * N_MOD * D_MODEL
    return {
        "x": jax.random.normal(ks[0], (BATCH, SEQ, D_MODEL), f32),
        "c": jax.random.normal(ks[1], (BATCH, D_MODEL), f32),
        "positions": jnp.tile(jnp.arange(SEQ, dtype=jnp.int32)[None, :], (BATCH, 1)),
        "ada_w": nrm(ks[2], (DEPTH, D_MODEL, n_mod), D_MODEL, ADA_INIT),
        "ada_b": small(ks[3], (DEPTH, n_mod)),
        "norm_pre": gain(ks[4], (DEPTH, N_SUBLAYERS, D_MODEL)),
        "norm_post": gain(ks[5], (DEPTH, N_SUBLAYERS, D_MODEL)),
        "ffn_w_gate": nrm(ks[6], (DEPTH, 2, D_MODEL, D_FF), D_MODEL),
        "ffn_w_up": nrm(ks[7], (DEPTH, 2, D_MODEL, D_FF), D_MODEL),
        "ffn_w_down": nrm(ks[8], (DEPTH, 2, D_FF, D_MODEL), D_FF),
        "pool_w": nrm(ks[9], (N_POOL_LAYERS, N_POOL_GROUPS, POOL_GROUP_DIM, POOL_GROUP_DIM), POOL_GROUP_DIM),
        "pool_b": small(ks[10], (N_POOL_LAYERS, N_POOL_GROUPS, POOL_GROUP_DIM)),
        "pool_scale": gain(ks[11], (N_POOL_LAYERS, D_MODEL), 0.1),
        "mla_w_dq": nrm(ks[12], (N_MLA_LAYERS, D_MODEL, Q_LORA_RANK), D_MODEL),
        "mla_q_norm": gain(ks[13], (N_MLA_LAYERS, Q_LORA_RANK)),
        "mla_w_uq": nrm(ks[14], (N_MLA_LAYERS, Q_LORA_RANK, MLA_HEADS * QK_HEAD_DIM), Q_LORA_RANK),
        "mla_w_dkv": nrm(ks[15], (N_MLA_LAYERS, D_MODEL, KV_LORA_RANK + QK_ROPE_DIM), D_MODEL),
        "mla_kv_norm": gain(ks[16], (N_MLA_LAYERS, KV_LORA_RANK)),
        "mla_w_ukv": nrm(ks[17], (N_MLA_LAYERS, KV_LORA_RANK, MLA_HEADS * (QK_NOPE_DIM + V_HEAD_DIM)), KV_LORA_RANK),
        "mla_w_o": nrm(ks[18], (N_MLA_LAYERS, MLA_HEADS * V_HEAD_DIM, D_MODEL), MLA_HEADS * V_HEAD_DIM),
    }


def reference(x, c, positions, ada_w, ada_b, norm_pre, norm_post, ffn_w_gate, ffn_w_up,
              ffn_w_down, pool_w, pool_b, pool_scale, mla_w_dq, mla_q_norm, mla_w_uq,
              mla_w_dkv, mla_kv_norm, mla_w_ukv, mla_w_o):
    b = x.shape[0]
    c_act = jax.nn.silu(c)
    for layer in range(DEPTH):
        mod = (c_act @ ada_w[layer] + ada_b[layer]).reshape(b, N_SUBLAYERS, N_MOD, D_MODEL)
        x = _sublayer(x, mod[:, 0], norm_pre[layer, 0], norm_post[layer, 0],
                      lambda h: _swiglu(h, ffn_w_gate[layer, 0], ffn_w_up[layer, 0], ffn_w_down[layer, 0]),
                      0.5)
        j = layer // N_MIXERS
        if layer % N_MIXERS == 0:
            mixer = lambda h: _pool_mixer(h, pool_w[j], pool_b[j], pool_scale[j])
        else:
            mixer = lambda h: _mla(h, positions, mla_w_dq[j], mla_q_norm[j], mla_w_uq[j],
                                   mla_w_dkv[j], mla_kv_norm[j], mla_w_ukv[j], mla_w_o[j])
        x = _sublayer(x, mod[:, 1], norm_pre[layer, 1], norm_post[layer, 1], mixer, 1.0)
        x = _sublayer(x, mod[:, 2], norm_pre[layer, 2], norm_post[layer, 2],
                      lambda h: _swiglu(h, ffn_w_gate[layer, 1], ffn_w_up[layer, 1], ffn_w_down[layer, 1]),
                      0.5)
    return x
```

```python
import functools

import jax
import jax.numpy as jnp
from jax import lax
from jax.experimental import pallas as pl
from jax.experimental.pallas import tpu as pltpu

D_MODEL = 2048
DEPTH = 4
N_SUBLAYERS = 3
N_MOD = 3
D_FF = 5632
POOL_WINDOWS = (2, 4, 8, 16)
POOL_GROUP_DIM = D_MODEL // len(POOL_WINDOWS)
POOL_HALO = 16
MLA_HEADS = 16
Q_LORA_RANK = 512
KV_LORA_RANK = 512
QK_NOPE_DIM = 128
QK_ROPE_DIM = 64
V_HEAD_DIM = 128
QK_HEAD_DIM = QK_NOPE_DIM + QK_ROPE_DIM
QK_PAD_DIM = 256
ROPE_THETA = 10000.0
NORM_EPS = 1e-6

F32 = jnp.float32
BF16 = jnp.bfloat16
MIB = 1024 * 1024


def _params(semantics, vmem_mib):
    return pltpu.CompilerParams(dimension_semantics=semantics, vmem_limit_bytes=vmem_mib * MIB)


def _rms(x, g):
    return (x * lax.rsqrt(jnp.mean(x * x, axis=-1, keepdims=True) + NORM_EPS)) * g


def _modulated_norm(x, mod_ref, gpre_ref):
    shift = mod_ref[0, 0:1, :]
    scale = mod_ref[0, 1:2, :]
    return _rms(x, gpre_ref[...]) * (1.0 + scale) + shift


def _gated_residual(x, y, mod_ref, gpost_ref, weight):
    gate = mod_ref[0, 2:3, :]
    return x + (weight * (1.0 + gate)) * _rms(y, gpost_ref[...])


def _mod_kernel(c_ref, w_ref, b_ref, o_ref):
    c = c_ref[...]
    c_act = (c * jax.nn.sigmoid(c)).astype(BF16)
    o_ref[0] = jnp.dot(c_act, w_ref[0].astype(BF16), preferred_element_type=F32) + b_ref[0]


def _modulation(c, ada_w, ada_b):
    b = c.shape[0]
    n_mod = ada_w.shape[-1]
    rows = 8
    c_pad = jnp.zeros((rows, D_MODEL), F32).at[:b].set(c)
    tn = 1024
    out = pl.pallas_call(
        _mod_kernel,
        grid=(DEPTH, n_mod // tn),
        in_specs=[
            pl.BlockSpec((rows, D_MODEL), lambda l, j: (0, 0)),
            pl.BlockSpec((1, D_MODEL, tn), lambda l, j: (l, 0, j)),
            pl.BlockSpec((1, 1, tn), lambda l, j: (l, 0, j)),
        ],
        out_specs=pl.BlockSpec((1, rows, tn), lambda l, j: (l, 0, j)),
        out_shape=jax.ShapeDtypeStruct((DEPTH, rows, n_mod), F32),
        compiler_params=_params(("parallel", "parallel"), 40),
        name="adaln_mod",
    )(c_pad, ada_w, ada_b.reshape(DEPTH, 1, n_mod))
    return out[:, :b].reshape(DEPTH, b, N_SUBLAYERS, N_MOD, D_MODEL)


def _ffn_kernel(x_ref, mod_ref, gpre_ref, gpost_ref, wg_ref, wu_ref, wd_ref, o_ref, h_ref, *, weight):
    j = pl.program_id(1)

    @pl.when(j == 0)
    def _():
        h_ref[...] = _modulated_norm(x_ref[...], mod_ref, gpre_ref).astype(BF16)

    h = h_ref[...]
    g = jnp.dot(h, wg_ref[...], preferred_element_type=F32)
    u = jnp.dot(h, wu_ref[...], preferred_element_type=F32)
    a = ((g * jax.nn.sigmoid(g)) * u).astype(BF16)
    part = jnp.dot(a, wd_ref[...], preferred_element_type=F32)

    @pl.when(j == 0)
    def _():
        o_ref[...] = part

    @pl.when(j > 0)
    def _():
        o_ref[...] += part

    @pl.when(j == pl.num_programs(1) - 1)
    def _():
        o_ref[...] = _gated_residual(x_ref[...], o_ref[...], mod_ref, gpost_ref, weight)


def _ffn_sublayer(x2, mod, g_pre, g_post, wg, wu, wd, seq, weight, tm=512, tf=512):
    t = x2.shape[0]
    tiles_per_seq = seq // tm
    return pl.pallas_call(
        functools.partial(_ffn_kernel, weight=weight),
        grid=(t // tm, D_FF // tf),
        in_specs=[
            pl.BlockSpec((tm, D_MODEL), lambda i, j: (i, 0)),
            pl.BlockSpec((1, N_MOD, D_MODEL), lambda i, j: (i // tiles_per_seq, 0, 0)),
            pl.BlockSpec((1, D_MODEL), lambda i, j: (0, 0)),
            pl.BlockSpec((1, D_MODEL), lambda i, j: (0, 0)),
            pl.BlockSpec((D_MODEL, tf), lambda i, j: (0, j)),
            pl.BlockSpec((D_MODEL, tf), lambda i, j: (0, j)),
            pl.BlockSpec((tf, D_MODEL), lambda i, j: (j, 0)),
        ],
        out_specs=pl.BlockSpec((tm, D_MODEL), lambda i, j: (i, 0)),
        out_shape=jax.ShapeDtypeStruct((t, D_MODEL), F32),
        scratch_shapes=[pltpu.VMEM((tm, D_MODEL), BF16)],
        compiler_params=_params(("parallel", "arbitrary"), 48),
        name="ffn_sublayer",
    )(x2, mod, g_pre.reshape(1, D_MODEL), g_post.reshape(1, D_MODEL), wg, wu, wd)


def _pool_kernel(x_ref, mod_ref, gpre_ref, gpost_ref, w_ref, b_ref, cs_ref, o_ref, hext_ref, *, tm):
    s = pl.program_id(1)

    @pl.when(s == 0)
    def _():
        hext_ref[0:POOL_HALO, :] = jnp.zeros((POOL_HALO, D_MODEL), F32)

    x = x_ref[0]
    hext_ref[POOL_HALO:POOL_HALO + tm, :] = _modulated_norm(x, mod_ref, gpre_ref)
    t_idx = s * tm + lax.broadcasted_iota(jnp.int32, (tm, 1), 0)
    outs = []
    for g, w in enumerate(POOL_WINDOWS):
        cols = pl.ds(g * POOL_GROUP_DIM, POOL_GROUP_DIM)
        hg = hext_ref[pl.ds(POOL_HALO, tm), cols]
        win = hg
        for k in range(1, w):
            win = win + hext_ref[pl.ds(POOL_HALO - k, tm), cols]
        count = jnp.minimum(t_idx + 1, w).astype(F32)
        pooled = win / count - hg
        outs.append(jnp.dot(pooled.astype(BF16), w_ref[g], preferred_element_type=F32) + b_ref[g])
    y = jnp.concatenate(outs, axis=-1) * cs_ref[...]
    o_ref[0] = _gated_residual(x, y, mod_ref, gpost_ref, 1.0)
    hext_ref[0:POOL_HALO, :] = hext_ref[tm:tm + POOL_HALO, :]


def _pool_sublayer(x, mod, g_pre, g_post, w, bias, ch_scale, tm=512):
    b, seq, _ = x.shape
    n_g = len(POOL_WINDOWS)
    return pl.pallas_call(
        functools.partial(_pool_kernel, tm=tm),
        grid=(b, seq // tm),
        in_specs=[
            pl.BlockSpec((1, tm, D_MODEL), lambda i, s: (i, s, 0)),
            pl.BlockSpec((1, N_MOD, D_MODEL), lambda i, s: (i, 0, 0)),
            pl.BlockSpec((1, D_MODEL), lambda i, s: (0, 0)),
            pl.BlockSpec((1, D_MODEL), lambda i, s: (0, 0)),
            pl.BlockSpec((n_g, POOL_GROUP_DIM, POOL_GROUP_DIM), lambda i, s: (0, 0, 0)),
            pl.BlockSpec((n_g, 1, POOL_GROUP_DIM), lambda i, s: (0, 0, 0)),
            pl.BlockSpec((1, D_MODEL), lambda i, s: (0, 0)),
        ],
        out_specs=pl.BlockSpec((1, tm, D_MODEL), lambda i, s: (i, s, 0)),
        out_shape=jax.ShapeDtypeStruct(x.shape, F32),
        scratch_shapes=[pltpu.VMEM((POOL_HALO + tm, D_MODEL), F32)],
        compiler_params=_params(("arbitrary", "arbitrary"), 48),
        name="pool_sublayer",
    )(x, mod, g_pre.reshape(1, D_MODEL), g_post.reshape(1, D_MODEL), w.astype(BF16),
      bias.reshape(n_g, 1, POOL_GROUP_DIM), ch_scale.reshape(1, D_MODEL))


def _mla_proj_kernel(x_ref, mod_ref, gpre_ref, pos_ref, freq_ref, sign_ref, wdq_ref, qn_ref, wq_ref,
                     wqs_ref, wdkv_ref, kvn_ref, wukv_ref, q_ref, k_ref, v_ref):
    h = _modulated_norm(x_ref[0], mod_ref, gpre_ref).astype(BF16)
    ang = pos_ref[0].astype(F32) * freq_ref[...]
    cos = jnp.cos(ang)
    sin = jnp.sin(ang) * sign_ref[...]
    sm_scale = QK_HEAD_DIM ** -0.5

    c_q = _rms(jnp.dot(h, wdq_ref[...], preferred_element_type=F32), qn_ref[...]).astype(BF16)
    qp = jnp.dot(c_q, wq_ref[...], preferred_element_type=F32)
    qs = jnp.dot(c_q, wqs_ref[...], preferred_element_type=F32)
    for p in range(MLA_HEADS // 2):
        base = 3 * 128 * p
        nope_a = qp[:, base:base + 128] * sm_scale
        rope = (qp[:, base + 128:base + 256] * cos + qs[:, 128 * p:128 * (p + 1)] * sin) * sm_scale
        nope_b = qp[:, base + 256:base + 384] * sm_scale
        q_ref[0, 2 * p, :, 0:128] = nope_a.astype(BF16)
        q_ref[0, 2 * p, :, 128:256] = rope.astype(BF16)
        q_ref[0, 2 * p + 1, :, 0:128] = rope.astype(BF16)
        q_ref[0, 2 * p + 1, :, 128:256] = nope_b.astype(BF16)

    ckv = jnp.dot(h, wdkv_ref[...], preferred_element_type=F32)
    n0 = KV_LORA_RANK
    kr_a = (ckv[:, n0:n0 + 128] * cos + ckv[:, n0 + 256:n0 + 384] * sin).astype(BF16)
    kr_b = (ckv[:, n0 + 128:n0 + 256] * cos + ckv[:, n0 + 384:n0 + 512] * sin).astype(BF16)
    c_kv = _rms(ckv[:, :n0], kvn_ref[...]).astype(BF16)
    kv = jnp.dot(c_kv, wukv_ref[...], preferred_element_type=F32)
    v0 = MLA_HEADS * QK_NOPE_DIM
    for hd in range(MLA_HEADS):
        k_nope = kv[:, 128 * hd:128 * (hd + 1)].astype(BF16)
        if hd % 2 == 0:
            k_ref[0, hd, :, 0:128] = k_nope
            k_ref[0, hd, :, 128:256] = kr_a
        else:
            k_ref[0, hd, :, 0:128] = kr_b
            k_ref[0, hd, :, 128:256] = k_nope
        v_ref[0, hd] = kv[:, v0 + 128 * hd:v0 + 128 * (hd + 1)].astype(BF16)


def _attn_kernel(q_ref, k_ref, v_ref, o_ref, *, tq):
    i = pl.program_id(2)
    q = q_ref[0, 0]
    neg = jnp.finfo(F32).min

    def step(j, carry, masked):
        m, l, acc = carry
        k = k_ref[0, 0, pl.ds(pl.multiple_of(j * tq, tq), tq), :]
        v = v_ref[0, 0, pl.ds(pl.multiple_of(j * tq, tq), tq), :]
        s = lax.dot_general(q, k, (((1,), (1,)), ((), ())), preferred_element_type=F32)
        if masked:
            row = lax.broadcasted_iota(jnp.int32, (tq, tq), 0)
            col = lax.broadcasted_iota(jnp.int32, (tq, tq), 1)
            s = jnp.where(row >= col, s, neg)
        m_new = jnp.maximum(m, jnp.max(s, axis=-1, keepdims=True))
        alpha = jnp.exp(m - m_new)
        p = jnp.exp(s - m_new)
        l = alpha * l + jnp.sum(p, axis=-1, keepdims=True)
        acc = alpha * acc + jnp.dot(p.astype(BF16), v, preferred_element_type=F32)
        return m_new, l, acc

    init = (jnp.full((tq, 1), neg, F32), jnp.zeros((tq, 1), F32), jnp.zeros((tq, V_HEAD_DIM), F32))
    carry = lax.fori_loop(0, i, lambda j, c: step(j, c, False), init)
    _, l, acc = step(i, carry, True)
    o_ref[0] = (acc / l).astype(BF16)


def _mla_out_kernel(x_ref, a_ref, mod_ref, gpost_ref, wo_ref, o_ref):
    y = jnp.dot(a_ref[...], wo_ref[...], preferred_element_type=F32)
    o_ref[...] = _gated_residual(x_ref[...], y, mod_ref, gpost_ref, 1.0)


def _mla_weights(w_dq, w_uq, w_dkv, w_ukv, w_o):
    half = QK_ROPE_DIM // 2
    uq = w_uq.reshape(Q_LORA_RANK, MLA_HEADS, QK_HEAD_DIM)
    nope = uq[:, :, :QK_NOPE_DIM]
    rope = uq[:, :, QK_NOPE_DIM:]
    rope_sw = jnp.concatenate([rope[:, :, half:], rope[:, :, :half]], axis=-1)
    pair = lambda a: a.reshape(Q_LORA_RANK, MLA_HEADS // 2, -1)
    wq = jnp.concatenate([nope[:, 0::2], pair(rope), nope[:, 1::2]], axis=-1)
    wq = wq.reshape(Q_LORA_RANK, -1)
    wqs = pair(rope_sw).reshape(Q_LORA_RANK, -1)
    kr = w_dkv[:, KV_LORA_RANK:]
    kr_sw = jnp.concatenate([kr[:, half:], kr[:, :half]], axis=-1)
    zero = jnp.zeros_like(kr)
    wdkv = jnp.concatenate([w_dkv[:, :KV_LORA_RANK], kr, zero, zero, kr, kr_sw, zero, zero, kr_sw], axis=-1)
    ukv = w_ukv.reshape(KV_LORA_RANK, MLA_HEADS, QK_NOPE_DIM + V_HEAD_DIM)
    wukv = jnp.concatenate([ukv[:, :, :QK_NOPE_DIM].reshape(KV_LORA_RANK, -1),
                            ukv[:, :, QK_NOPE_DIM:].reshape(KV_LORA_RANK, -1)], axis=-1)
    return (w_dq.astype(BF16), wq.astype(BF16), wqs.astype(BF16), wdkv.astype(BF16), wukv.astype(BF16),
            w_o.astype(BF16))


def _mla_sublayer(x, mod, g_pre, g_post, positions, w_dq, q_norm, w_uq, w_dkv, kv_norm, w_ukv, w_o,
                  tm=256, tq=512):
    b, seq, _ = x.shape
    wdq, wq, wqs, wdkv, wukv, wo = _mla_weights(w_dq, w_uq, w_dkv, w_ukv, w_o)
    half = QK_ROPE_DIM // 2
    inv_freq = ROPE_THETA ** (-jnp.arange(0, QK_ROPE_DIM, 2, dtype=F32) / QK_ROPE_DIM)
    freq = jnp.tile(inv_freq, 4).reshape(1, 128)
    sign = jnp.tile(jnp.concatenate([-jnp.ones((half,), F32), jnp.ones((half,), F32)]), 2).reshape(1, 128)
    full = lambda a: pl.BlockSpec(a.shape, lambda i, s: (0,) * a.ndim)
    qn = q_norm.reshape(1, -1)
    kvn = kv_norm.reshape(1, -1)
    gpre = g_pre.reshape(1, D_MODEL)
    q, k, v = pl.pallas_call(
        _mla_proj_kernel,
        grid=(b, seq // tm),
        in_specs=[
            pl.BlockSpec((1, tm, D_MODEL), lambda i, s: (i, s, 0)),
            pl.BlockSpec((1, N_MOD, D_MODEL), lambda i, s: (i, 0, 0)),
            full(gpre),
            pl.BlockSpec((1, tm, 1), lambda i, s: (i, s, 0)),
            full(freq), full(sign), full(wdq), full(qn), full(wq), full(wqs), full(wdkv), full(kvn),
            full(wukv),
        ],
        out_specs=[
            pl.BlockSpec((1, MLA_HEADS, tm, QK_PAD_DIM), lambda i, s: (i, 0, s, 0)),
            pl.BlockSpec((1, MLA_HEADS, tm, QK_PAD_DIM), lambda i, s: (i, 0, s, 0)),
            pl.BlockSpec((1, MLA_HEADS, tm, V_HEAD_DIM), lambda i, s: (i, 0, s, 0)),
        ],
        out_shape=[
            jax.ShapeDtypeStruct((b, MLA_HEADS, seq, QK_PAD_DIM), BF16),
            jax.ShapeDtypeStruct((b, MLA_HEADS, seq, QK_PAD_DIM), BF16),
            jax.ShapeDtypeStruct((b, MLA_HEADS, seq, V_HEAD_DIM), BF16),
        ],
        compiler_params=_params(("parallel", "parallel"), 56),
        name="mla_proj",
    )(x, mod, gpre, positions.reshape(b, seq, 1), freq, sign, wdq, qn, wq, wqs, wdkv, kvn, wukv)

    attn = pl.pallas_call(
        functools.partial(_attn_kernel, tq=tq),
        grid=(b, MLA_HEADS, seq // tq),
        in_specs=[
            pl.BlockSpec((1, 1, tq, QK_PAD_DIM), lambda i, hd, s: (i, hd, s, 0)),
            pl.BlockSpec((1, 1, seq, QK_PAD_DIM), lambda i, hd, s: (i, hd, 0, 0)),
            pl.BlockSpec((1, 1, seq, V_HEAD_DIM), lambda i, hd, s: (i, hd, 0, 0)),
        ],
        out_specs=pl.BlockSpec((1, tq, V_HEAD_DIM), lambda i, hd, s: (i, s, hd)),
        out_shape=jax.ShapeDtypeStruct((b, seq, MLA_HEADS * V_HEAD_DIM), BF16),
        compiler_params=_params(("parallel", "parallel", "parallel"), 40),
        name="mla_attention",
    )(q, k, v)

    t = b * seq
    tmo = 512
    tiles_per_seq = seq // tmo
    out = pl.pallas_call(
        _mla_out_kernel,
        grid=(t // tmo,),
        in_specs=[
            pl.BlockSpec((tmo, D_MODEL), lambda i: (i, 0)),
            pl.BlockSpec((tmo, MLA_HEADS * V_HEAD_DIM), lambda i: (i, 0)),
            pl.BlockSpec((1, N_MOD, D_MODEL), lambda i: (i // tiles_per_seq, 0, 0)),
            pl.BlockSpec((1, D_MODEL), lambda i: (0, 0)),
            pl.BlockSpec(wo.shape, lambda i: (0, 0)),
        ],
        out_specs=pl.BlockSpec((tmo, D_MODEL), lambda i: (i, 0)),
        out_shape=jax.ShapeDtypeStruct((t, D_MODEL), F32),
        compiler_params=_params(("parallel",), 48),
        name="mla_out",
    )(x.reshape(t, D_MODEL), attn.reshape(t, -1), mod, g_post.reshape(1, D_MODEL), wo)
    return out.reshape(b, seq, D_MODEL)


def kernel(x, c, positions, ada_w, ada_b, norm_pre, norm_post, ffn_w_gate, ffn_w_up, ffn_w_down, pool_w, pool_b, pool_scale, mla_w_dq, mla_q_norm, mla_w_uq, mla_w_dkv, mla_kv_norm, mla_w_ukv, mla_w_o):
    b, seq, _ = x.shape
    mod = _modulation(c, ada_w, ada_b)
    wg = ffn_w_gate.astype(BF16)
    wu = ffn_w_up.astype(BF16)
    wd = ffn_w_down.astype(BF16)

    def ffn(x, layer, sub, which):
        y = _ffn_sublayer(x.reshape(b * seq, D_MODEL), mod[layer, :, sub], norm_pre[layer, sub],
                          norm_post[layer, sub], wg[layer, which], wu[layer, which], wd[layer, which],
                          seq, 0.5)
        return y.reshape(b, seq, D_MODEL)

    for layer in range(DEPTH):
        x = ffn(x, layer, 0, 0)
        j = layer // 2
        if layer % 2 == 0:
            x = _pool_sublayer(x, mod[layer, :, 1], norm_pre[layer, 1], norm_post[layer, 1],
                               pool_w[j], pool_b[j], pool_scale[j])
        else:
            x = _mla_sublayer(x, mod[layer, :, 1], norm_pre[layer, 1], norm_post[layer, 1], positions,
                              mla_w_dq[j], mla_q_norm[j], mla_w_uq[j], mla_w_dkv[j], mla_kv_norm[j],
                              mla_w_ukv[j], mla_w_o[j])
        x = ffn(x, layer, 2, 1)
    return x
```

```python
import functools
import math

import jax
import jax.numpy as jnp
from jax import lax
from jax.experimental import pallas as pl
from jax.experimental.pallas import tpu as pltpu

D_MODEL = 2048
DEPTH = 4
N_SUBLAYERS = 3
N_MOD = 3
D_FF = 5632
POOL_WINDOWS = (2, 4, 8, 16)
POOL_GROUP_DIM = D_MODEL // len(POOL_WINDOWS)
POOL_HALO = 16
MLA_HEADS = 16
Q_LORA_RANK = 512
KV_LORA_RANK = 512
QK_NOPE_DIM = 128
QK_ROPE_DIM = 64
ROPE_HALF = QK_ROPE_DIM // 2
V_HEAD_DIM = 128
QK_HEAD_DIM = QK_NOPE_DIM + QK_ROPE_DIM
QK_PAD_DIM = 256
ROPE_THETA = 10000.0
NORM_EPS = 1e-6
ROW_CHUNK = 128

F32 = jnp.float32
BF16 = jnp.bfloat16
MIB = 1024 * 1024


def _params(semantics, vmem_mib):
    return pltpu.CompilerParams(dimension_semantics=semantics, vmem_limit_bytes=vmem_mib * MIB)


def _resident(a):
    return pl.BlockSpec(a.shape, lambda *_: (0,) * a.ndim, pipeline_mode=pl.Buffered(1))


def _rms(x, g):
    return (x * lax.rsqrt(jnp.mean(x * x, axis=-1, keepdims=True) + NORM_EPS)) * g


def _modulated_norm(x, mod_ref, gpre_ref):
    shift = mod_ref[0, 0:1, :]
    scale = mod_ref[0, 1:2, :]
    return _rms(x, gpre_ref[...]) * (1.0 + scale) + shift


def _gated_residual(x, y, mod_ref, gpost_ref, weight):
    gate = mod_ref[0, 2:3, :]
    return x + (weight * (1.0 + gate)) * _rms(y, gpost_ref[...])


def _mod_kernel(c_ref, w_ref, b_ref, o_ref):
    c = c_ref[...]
    c_act = (c * jax.nn.sigmoid(c)).astype(BF16)
    o_ref[0] = jnp.dot(c_act, w_ref[0].astype(BF16), preferred_element_type=F32) + b_ref[0]


def _modulation(c, ada_w, ada_b):
    b = c.shape[0]
    n_mod = ada_w.shape[-1]
    rows = 8
    c_pad = jnp.zeros((rows, D_MODEL), F32).at[:b].set(c)
    tn = 1024
    out = pl.pallas_call(
        _mod_kernel,
        grid=(DEPTH, n_mod // tn),
        in_specs=[
            pl.BlockSpec((rows, D_MODEL), lambda l, j: (0, 0)),
            pl.BlockSpec((1, D_MODEL, tn), lambda l, j: (l, 0, j)),
            pl.BlockSpec((1, 1, tn), lambda l, j: (l, 0, j)),
        ],
        out_specs=pl.BlockSpec((1, rows, tn), lambda l, j: (l, 0, j)),
        out_shape=jax.ShapeDtypeStruct((DEPTH, rows, n_mod), F32),
        compiler_params=_params(("parallel", "parallel"), 40),
        name="adaln_mod",
    )(c_pad, ada_w, ada_b.reshape(DEPTH, 1, n_mod))
    return out[:, :b].reshape(DEPTH, b, N_SUBLAYERS, N_MOD, D_MODEL)


def _ffn_kernel(x_ref, mod_ref, gpre_ref, gpost_ref, wg_ref, wu_ref, wd_ref, o_ref, h_ref, *, weight):
    j = pl.program_id(1)
    n_chunks = x_ref.shape[0] // ROW_CHUNK

    def rows_of(c):
        return pl.ds(pl.multiple_of(c * ROW_CHUNK, ROW_CHUNK), ROW_CHUNK)

    @pl.when(j == 0)
    def _():
        def chunk(c, carry):
            rows = rows_of(c)
            h_ref[rows, :] = _modulated_norm(x_ref[rows, :], mod_ref, gpre_ref).astype(BF16)
            return carry
        lax.fori_loop(0, n_chunks, chunk, 0)

    h = h_ref[...]
    g = jnp.dot(h, wg_ref[0, 0], preferred_element_type=F32)
    u = jnp.dot(h, wu_ref[0, 0], preferred_element_type=F32)
    a = ((g * jax.nn.sigmoid(g)) * u).astype(BF16)

    @pl.when(j == 0)
    def _():
        o_ref[...] = jnp.dot(a, wd_ref[0, 0], preferred_element_type=F32)

    @pl.when(j > 0)
    def _():
        o_ref[...] += jnp.dot(a, wd_ref[0, 0], preferred_element_type=F32)

    @pl.when(j == pl.num_programs(1) - 1)
    def _():
        def chunk(c, carry):
            rows = rows_of(c)
            o_ref[rows, :] = _gated_residual(x_ref[rows, :], o_ref[rows, :], mod_ref, gpost_ref, weight)
            return carry
        lax.fori_loop(0, n_chunks, chunk, 0)


def _ffn_sublayer(x2, mod, g_pre, g_post, wg, wu, wd, layer, which, seq, weight, tm=1024, tf=512):
    t = x2.shape[0]
    tiles_per_seq = seq // tm
    return pl.pallas_call(
        functools.partial(_ffn_kernel, weight=weight),
        grid=(t // tm, D_FF // tf),
        in_specs=[
            pl.BlockSpec((tm, D_MODEL), lambda i, j: (i, 0)),
            pl.BlockSpec((1, N_MOD, D_MODEL), lambda i, j: (i // tiles_per_seq, 0, 0)),
            pl.BlockSpec((1, D_MODEL), lambda i, j: (0, 0)),
            pl.BlockSpec((1, D_MODEL), lambda i, j: (0, 0)),
            pl.BlockSpec((1, 1, D_MODEL, tf), lambda i, j: (layer, which, 0, j)),
            pl.BlockSpec((1, 1, D_MODEL, tf), lambda i, j: (layer, which, 0, j)),
            pl.BlockSpec((1, 1, tf, D_MODEL), lambda i, j: (layer, which, j, 0)),
        ],
        out_specs=pl.BlockSpec((tm, D_MODEL), lambda i, j: (i, 0)),
        out_shape=jax.ShapeDtypeStruct((t, D_MODEL), F32),
        scratch_shapes=[pltpu.VMEM((tm, D_MODEL), BF16)],
        compiler_params=_params(("parallel", "arbitrary"), 62),
        name="ffn_sublayer",
    )(x2, mod, g_pre.reshape(1, D_MODEL), g_post.reshape(1, D_MODEL), wg, wu, wd)


def _pool_kernel(x_ref, mod_ref, gpre_ref, gpost_ref, w_ref, b_ref, cs_ref, o_ref, hext_ref, *, tm):
    s = pl.program_id(1)

    @pl.when(s == 0)
    def _():
        hext_ref[0:POOL_HALO, :] = jnp.zeros((POOL_HALO, D_MODEL), F32)

    x = x_ref[0]
    hext_ref[POOL_HALO:POOL_HALO + tm, :] = _modulated_norm(x, mod_ref, gpre_ref)
    t_idx = s * tm + lax.broadcasted_iota(jnp.int32, (tm, 1), 0)
    outs = []
    for g, w in enumerate(POOL_WINDOWS):
        cols = pl.ds(g * POOL_GROUP_DIM, POOL_GROUP_DIM)
        hg = hext_ref[pl.ds(POOL_HALO, tm), cols]
        win = hg
        for k in range(1, w):
            win = win + hext_ref[pl.ds(POOL_HALO - k, tm), cols]
        count = jnp.minimum(t_idx + 1, w).astype(F32)
        pooled = win / count - hg
        outs.append(jnp.dot(pooled.astype(BF16), w_ref[g], preferred_element_type=F32) + b_ref[g])
    y = jnp.concatenate(outs, axis=-1) * cs_ref[...]
    o_ref[0] = _gated_residual(x, y, mod_ref, gpost_ref, 1.0)
    hext_ref[0:POOL_HALO, :] = hext_ref[tm:tm + POOL_HALO, :]


def _pool_sublayer(x, mod, g_pre, g_post, w, bias, ch_scale, tm=512):
    b, seq, _ = x.shape
    n_g = len(POOL_WINDOWS)
    return pl.pallas_call(
        functools.partial(_pool_kernel, tm=tm),
        grid=(b, seq // tm),
        in_specs=[
            pl.BlockSpec((1, tm, D_MODEL), lambda i, s: (i, s, 0)),
            pl.BlockSpec((1, N_MOD, D_MODEL), lambda i, s: (i, 0, 0)),
            pl.BlockSpec((1, D_MODEL), lambda i, s: (0, 0)),
            pl.BlockSpec((1, D_MODEL), lambda i, s: (0, 0)),
            pl.BlockSpec((n_g, POOL_GROUP_DIM, POOL_GROUP_DIM), lambda i, s: (0, 0, 0)),
            pl.BlockSpec((n_g, 1, POOL_GROUP_DIM), lambda i, s: (0, 0, 0)),
            pl.BlockSpec((1, D_MODEL), lambda i, s: (0, 0)),
        ],
        out_specs=pl.BlockSpec((1, tm, D_MODEL), lambda i, s: (i, s, 0)),
        out_shape=jax.ShapeDtypeStruct(x.shape, F32),
        scratch_shapes=[pltpu.VMEM((POOL_HALO + tm, D_MODEL), F32)],
        compiler_params=_params(("arbitrary", "arbitrary"), 48),
        name="pool_sublayer",
    )(x, mod, g_pre.reshape(1, D_MODEL), g_post.reshape(1, D_MODEL), w.astype(BF16),
      bias.reshape(n_g, 1, POOL_GROUP_DIM), ch_scale.reshape(1, D_MODEL))


def _mla_proj_kernel(x_ref, mod_ref, gpre_ref, posc_ref, posr_ref, freqr_ref, freqc_ref, sign_ref,
                     wdq_ref, qn_ref, wqt_ref, wdkv_ref, kvn_ref, wuk_ref, wvt_ref,
                     q_ref, k_ref, v_ref):
    h = _modulated_norm(x_ref[0], mod_ref, gpre_ref).astype(BF16)
    contract_last = (((1,), (1,)), ((), ()))
    q_scale = QK_HEAD_DIM ** -0.5 * math.log2(math.e)

    c_q = _rms(jnp.dot(h, wdq_ref[...], preferred_element_type=F32), qn_ref[...]).astype(BF16)
    qt = lax.dot_general(wqt_ref[...], c_q, contract_last, preferred_element_type=F32) * q_scale
    ang_t = freqc_ref[...] * posr_ref[0].astype(F32)
    cos_t = jnp.cos(ang_t)
    sin_t = jnp.sin(ang_t)
    for hd in range(MLA_HEADS):
        base = QK_PAD_DIM * hd
        r0 = base + QK_NOPE_DIM
        x1 = qt[r0:r0 + ROPE_HALF]
        x2 = qt[r0 + ROPE_HALF:r0 + QK_ROPE_DIM]
        q_ref[0, hd, 0:QK_NOPE_DIM, :] = qt[base:r0].astype(BF16)
        q_ref[0, hd, QK_NOPE_DIM:QK_NOPE_DIM + ROPE_HALF, :] = (x1 * cos_t - x2 * sin_t).astype(BF16)
        q_ref[0, hd, QK_NOPE_DIM + ROPE_HALF:QK_HEAD_DIM, :] = (x2 * cos_t + x1 * sin_t).astype(BF16)
        q_ref[0, hd, QK_HEAD_DIM:QK_PAD_DIM, :] = qt[base + QK_HEAD_DIM:base + QK_PAD_DIM].astype(BF16)

    ckv = jnp.dot(h, wdkv_ref[...], preferred_element_type=F32)
    n0 = KV_LORA_RANK
    ang = posc_ref[0].astype(F32) * freqr_ref[...]
    k_rope = (ckv[:, n0:n0 + 128] * jnp.cos(ang) + ckv[:, n0 + 128:n0 + 256] * (jnp.sin(ang) * sign_ref[...]))
    k_rope = k_rope.astype(BF16)
    c_kv = _rms(ckv[:, :n0], kvn_ref[...]).astype(BF16)
    k_nope = jnp.dot(c_kv, wuk_ref[...], preferred_element_type=F32)
    vt = lax.dot_general(wvt_ref[...], c_kv, contract_last, preferred_element_type=F32)
    for hd in range(MLA_HEADS):
        k_ref[0, hd, :, 0:128] = k_nope[:, 128 * hd:128 * (hd + 1)].astype(BF16)
        k_ref[0, hd, :, 128:256] = k_rope
        v_ref[0, hd] = vt[128 * hd:128 * (hd + 1)].astype(BF16)


def _attn_kernel(qt_ref, k_ref, vt_ref, o_ref, m_ref, l_ref, acc_ref, st0_ref, *, tq, heads):
    i = pl.program_id(2)
    neg = jnp.finfo(F32).min
    m_ref[...] = jnp.full_like(m_ref, neg)
    l_ref[...] = jnp.zeros_like(l_ref)
    acc_ref[...] = jnp.zeros_like(acc_ref)

    def scores(g, j):
        start = pl.multiple_of(j * tq, tq)
        return jnp.dot(k_ref[0, g, pl.ds(start, tq), :], qt_ref[0, g], preferred_element_type=F32)

    st0_ref[...] = scores(0, 0)

    def step(j, masked):
        start = pl.multiple_of(j * tq, tq)
        st_next = st0_ref[...]
        for g in range(heads):
            st = st_next
            if g + 1 < heads:
                st_next = scores(g + 1, j)
            elif not masked:
                st0_ref[...] = scores(0, j + 1)
            vt = vt_ref[0, g, :, pl.ds(start, tq)]
            if masked:
                key = lax.broadcasted_iota(jnp.int32, (tq, tq), 0)
                qry = lax.broadcasted_iota(jnp.int32, (tq, tq), 1)
                st = jnp.where(qry >= key, st, neg)
            m_prev = m_ref[g]
            m_new = jnp.maximum(m_prev, jnp.max(st, axis=0, keepdims=True))
            alpha = jnp.exp2(m_prev - m_new)
            p = jnp.exp2(st - m_new)
            l_ref[g] = alpha * l_ref[g] + jnp.sum(p, axis=0, keepdims=True)
            acc_ref[g] = alpha * acc_ref[g] + jnp.dot(vt, p.astype(BF16), preferred_element_type=F32)
            m_ref[g] = m_new

    def body(j, carry):
        step(j, False)
        return carry

    lax.fori_loop(0, i, body, 0)
    step(i, True)
    for g in range(heads):
        o_ref[0, :, V_HEAD_DIM * g:V_HEAD_DIM * (g + 1)] = (acc_ref[g] / l_ref[g]).T.astype(BF16)


def _mla_out_kernel(x_ref, a_ref, mod_ref, gpost_ref, wo_ref, o_ref):
    y = jnp.dot(a_ref[...], wo_ref[...], preferred_element_type=F32)
    o_ref[...] = _gated_residual(x_ref[...], y, mod_ref, gpost_ref, 1.0)


def _mla_weights(w_dq, w_uq, w_dkv, w_ukv, w_o):
    uq = w_uq.reshape(Q_LORA_RANK, MLA_HEADS, QK_HEAD_DIM)
    uq = jnp.pad(uq, ((0, 0), (0, 0), (0, QK_PAD_DIM - QK_HEAD_DIM)))
    wqt = uq.reshape(Q_LORA_RANK, MLA_HEADS * QK_PAD_DIM).T
    kr = w_dkv[:, KV_LORA_RANK:]
    kr_sw = jnp.concatenate([kr[:, ROPE_HALF:], kr[:, :ROPE_HALF]], axis=-1)
    zero = jnp.zeros_like(kr)
    wdkv = jnp.concatenate([w_dkv[:, :KV_LORA_RANK], kr, zero, kr_sw, zero], axis=-1)
    ukv = w_ukv.reshape(KV_LORA_RANK, MLA_HEADS, QK_NOPE_DIM + V_HEAD_DIM)
    wuk = ukv[:, :, :QK_NOPE_DIM].reshape(KV_LORA_RANK, -1)
    wvt = ukv[:, :, QK_NOPE_DIM:].reshape(KV_LORA_RANK, -1).T
    return tuple(a.astype(BF16) for a in (w_dq, wqt, wdkv, wuk, wvt, w_o))


def _mla_sublayer(x, mod, g_pre, g_post, positions, w_dq, q_norm, w_uq, w_dkv, kv_norm, w_ukv, w_o,
                  tm=256, tq=512, heads_per_step=4):
    b, seq, _ = x.shape
    wdq, wqt, wdkv, wuk, wvt, wo = _mla_weights(w_dq, w_uq, w_dkv, w_ukv, w_o)
    inv_freq = ROPE_THETA ** (-jnp.arange(0, QK_ROPE_DIM, 2, dtype=F32) / QK_ROPE_DIM)
    freq_row = jnp.tile(inv_freq, 4).reshape(1, 128)
    freq_col = inv_freq.reshape(ROPE_HALF, 1)
    sign = jnp.tile(jnp.concatenate([-jnp.ones((ROPE_HALF,), F32), jnp.ones((ROPE_HALF,), F32)]), 2)
    sign = sign.reshape(1, 128)
    qn = q_norm.reshape(1, -1)
    kvn = kv_norm.reshape(1, -1)
    gpre = g_pre.reshape(1, D_MODEL)
    qt, k, vt = pl.pallas_call(
        _mla_proj_kernel,
        grid=(b, seq // tm),
        in_specs=[
            pl.BlockSpec((1, tm, D_MODEL), lambda i, s: (i, s, 0)),
            pl.BlockSpec((1, N_MOD, D_MODEL), lambda i, s: (i, 0, 0)),
            _resident(gpre),
            pl.BlockSpec((1, tm, 1), lambda i, s: (i, s, 0)),
            pl.BlockSpec((1, 1, tm), lambda i, s: (i, 0, s)),
            _resident(freq_row), _resident(freq_col), _resident(sign), _resident(wdq), _resident(qn),
            _resident(wqt), _resident(wdkv), _resident(kvn), _resident(wuk), _resident(wvt),
        ],
        out_specs=[
            pl.BlockSpec((1, MLA_HEADS, QK_PAD_DIM, tm), lambda i, s: (i, 0, 0, s)),
            pl.BlockSpec((1, MLA_HEADS, tm, QK_PAD_DIM), lambda i, s: (i, 0, s, 0)),
            pl.BlockSpec((1, MLA_HEADS, V_HEAD_DIM, tm), lambda i, s: (i, 0, 0, s)),
        ],
        out_shape=[
            jax.ShapeDtypeStruct((b, MLA_HEADS, QK_PAD_DIM, seq), BF16),
            jax.ShapeDtypeStruct((b, MLA_HEADS, seq, QK_PAD_DIM), BF16),
            jax.ShapeDtypeStruct((b, MLA_HEADS, V_HEAD_DIM, seq), BF16),
        ],
        compiler_params=_params(("parallel", "parallel"), 56),
        name="mla_proj",
    )(x, mod, gpre, positions.reshape(b, seq, 1), positions.reshape(b, 1, seq), freq_row, freq_col, sign,
      wdq, qn, wqt, wdkv, kvn, wuk, wvt)

    hg = heads_per_step
    attn = pl.pallas_call(
        functools.partial(_attn_kernel, tq=tq, heads=hg),
        grid=(b, MLA_HEADS // hg, seq // tq),
        in_specs=[
            pl.BlockSpec((1, hg, QK_PAD_DIM, tq), lambda i, hd, s: (i, hd, 0, s)),
            pl.BlockSpec((1, hg, seq, QK_PAD_DIM), lambda i, hd, s: (i, hd, 0, 0)),
            pl.BlockSpec((1, hg, V_HEAD_DIM, seq), lambda i, hd, s: (i, hd, 0, 0)),
        ],
        out_specs=pl.BlockSpec((1, tq, hg * V_HEAD_DIM), lambda i, hd, s: (i, s, hd)),
        out_shape=jax.ShapeDtypeStruct((b, seq, MLA_HEADS * V_HEAD_DIM), BF16),
        scratch_shapes=[pltpu.VMEM((hg, 1, tq), F32), pltpu.VMEM((hg, 1, tq), F32),
                        pltpu.VMEM((hg, V_HEAD_DIM, tq), F32), pltpu.VMEM((tq, tq), F32)],
        compiler_params=_params(("parallel", "parallel", "arbitrary"), 48),
        name="mla_attention",
    )(qt, k, vt)

    t = b * seq
    tmo = 512
    tiles_per_seq = seq // tmo
    out = pl.pallas_call(
        _mla_out_kernel,
        grid=(t // tmo,),
        in_specs=[
            pl.BlockSpec((tmo, D_MODEL), lambda i: (i, 0)),
            pl.BlockSpec((tmo, MLA_HEADS * V_HEAD_DIM), lambda i: (i, 0)),
            pl.BlockSpec((1, N_MOD, D_MODEL), lambda i: (i // tiles_per_seq, 0, 0)),
            pl.BlockSpec((1, D_MODEL), lambda i: (0, 0)),
            _resident(wo),
        ],
        out_specs=pl.BlockSpec((tmo, D_MODEL), lambda i: (i, 0)),
        out_shape=jax.ShapeDtypeStruct((t, D_MODEL), F32),
        compiler_params=_params(("parallel",), 48),
        name="mla_out",
    )(x.reshape(t, D_MODEL), attn.reshape(t, -1), mod, g_post.reshape(1, D_MODEL), wo)
    return out.reshape(b, seq, D_MODEL)


def kernel(x, c, positions, ada_w, ada_b, norm_pre, norm_post, ffn_w_gate, ffn_w_up, ffn_w_down, pool_w, pool_b, pool_scale, mla_w_dq, mla_q_norm, mla_w_uq, mla_w_dkv, mla_kv_norm, mla_w_ukv, mla_w_o):
    b, seq, _ = x.shape
    mod = _modulation(c, ada_w, ada_b)
    wg = ffn_w_gate.astype(BF16)
    wu = ffn_w_up.astype(BF16)
    wd = ffn_w_down.astype(BF16)

    def ffn(x, layer, sub, which):
        y = _ffn_sublayer(x.reshape(b * seq, D_MODEL), mod[layer, :, sub], norm_pre[layer, sub],
                          norm_post[layer, sub], wg, wu, wd, layer, which, seq, 0.5)
        return y.reshape(b, seq, D_MODEL)

    for layer in range(DEPTH):
        x = ffn(x, layer, 0, 0)
        j = layer // 2
        if layer % 2 == 0:
            x = _pool_sublayer(x, mod[layer, :, 1], norm_pre[layer, 1], norm_post[layer, 1],
                               pool_w[j], pool_b[j], pool_scale[j])
        else:
            x = _mla_sublayer(x, mod[layer, :, 1], norm_pre[layer, 1], norm_post[layer, 1], positions,
                              mla_w_dq[j], mla_q_norm[j], mla_w_uq[j], mla_w_dkv[j], mla_kv_norm[j],
                              mla_w_ukv[j], mla_w_o[j])
        x = ffn(x, layer, 2, 1)
    return x
```

```python
import functools
import math

import jax
import jax.numpy as jnp
from jax import lax
from jax.experimental import pallas as pl
from jax.experimental.pallas import tpu as pltpu

D_MODEL = 2048
DEPTH = 4
N_SUBLAYERS = 3
N_MOD = 3
D_FF = 5632
POOL_WINDOWS = (2, 4, 8, 16)
POOL_GROUP_DIM = D_MODEL // len(POOL_WINDOWS)
POOL_HALO = 16
MLA_HEADS = 16
Q_LORA_RANK = 512
KV_LORA_RANK = 512
QK_NOPE_DIM = 128
QK_ROPE_DIM = 64
ROPE_HALF = QK_ROPE_DIM // 2
V_HEAD_DIM = 128
QK_HEAD_DIM = QK_NOPE_DIM + QK_ROPE_DIM
QK_PAD_DIM = 256
ROPE_THETA = 10000.0
NORM_EPS = 1e-6
FFN_ROW_CHUNK = 256

F32 = jnp.float32
BF16 = jnp.bfloat16
MIB = 1024 * 1024


def _params(semantics, vmem_mib):
    return pltpu.CompilerParams(dimension_semantics=semantics, vmem_limit_bytes=vmem_mib * MIB)


def _resident(a):
    return pl.BlockSpec(a.shape, lambda *_: (0,) * a.ndim, pipeline_mode=pl.Buffered(1))


def _rms(x, g):
    return (x * lax.rsqrt(jnp.mean(x * x, axis=-1, keepdims=True) + NORM_EPS)) * g


def _modulated_norm(x, mod_ref, gpre_ref):
    shift = mod_ref[0, 0:1, :]
    scale = mod_ref[0, 1:2, :]
    return _rms(x, gpre_ref[...]) * (1.0 + scale) + shift


def _gated_residual(x, y, mod_ref, gpost_ref, weight):
    gate = mod_ref[0, 2:3, :]
    return x + (weight * (1.0 + gate)) * _rms(y, gpost_ref[...])


def _mod_kernel(c_ref, w_ref, b_ref, o_ref):
    c = c_ref[...]
    c_act = (c * jax.nn.sigmoid(c)).astype(BF16)
    o_ref[0] = jnp.dot(c_act, w_ref[0].astype(BF16), preferred_element_type=F32) + b_ref[0]


def _modulation(c, ada_w, ada_b):
    b = c.shape[0]
    n_mod = ada_w.shape[-1]
    rows = 8
    c_pad = jnp.zeros((rows, D_MODEL), F32).at[:b].set(c)
    tn = 1024
    out = pl.pallas_call(
        _mod_kernel,
        grid=(DEPTH, n_mod // tn),
        in_specs=[
            pl.BlockSpec((rows, D_MODEL), lambda l, j: (0, 0)),
            pl.BlockSpec((1, D_MODEL, tn), lambda l, j: (l, 0, j)),
            pl.BlockSpec((1, 1, tn), lambda l, j: (l, 0, j)),
        ],
        out_specs=pl.BlockSpec((1, rows, tn), lambda l, j: (l, 0, j)),
        out_shape=jax.ShapeDtypeStruct((DEPTH, rows, n_mod), F32),
        compiler_params=_params(("parallel", "parallel"), 40),
        name="adaln_mod",
    )(c_pad, ada_w, ada_b.reshape(DEPTH, 1, n_mod))
    return out[:, :b].reshape(DEPTH, b, N_SUBLAYERS, N_MOD, D_MODEL)


def _ffn_kernel(*refs, weight, convert_next):
    if convert_next:
        (x_ref, mod_ref, gpre_ref, gpost_ref, wg_ref, wu_ref, wd_ref, ng_ref, nu_ref, nd_ref,
         o_ref, og_ref, ou_ref, od_ref, h_ref) = refs
        og_ref[...] = ng_ref[0, 0].astype(BF16)
        ou_ref[...] = nu_ref[0, 0].astype(BF16)
        od_ref[...] = nd_ref[0, 0].astype(BF16)
    else:
        x_ref, mod_ref, gpre_ref, gpost_ref, wg_ref, wu_ref, wd_ref, o_ref, h_ref = refs
    j = pl.program_id(1)
    last = pl.num_programs(1) - 1
    row_chunks = [slice(r, r + FFN_ROW_CHUNK) for r in range(0, x_ref.shape[0], FFN_ROW_CHUNK)]

    def gate_up(h):
        g = jnp.dot(h, wg_ref[...], preferred_element_type=F32)
        u = jnp.dot(h, wu_ref[...], preferred_element_type=F32)
        return ((g * jax.nn.sigmoid(g)) * u).astype(BF16)

    def down(a):
        return jnp.dot(a, wd_ref[...], preferred_element_type=F32)

    @pl.when(j == 0)
    def _():
        for rows in row_chunks:
            h = _modulated_norm(x_ref[rows, :], mod_ref, gpre_ref).astype(BF16)
            h_ref[rows, :] = h
            o_ref[rows, :] = down(gate_up(h))

    @pl.when(jnp.logical_and(j > 0, j < last))
    def _():
        o_ref[...] += down(gate_up(h_ref[...]))

    @pl.when(j == last)
    def _():
        for rows in row_chunks:
            y = o_ref[rows, :] + down(gate_up(h_ref[rows, :]))
            o_ref[rows, :] = _gated_residual(x_ref[rows, :], y, mod_ref, gpost_ref, weight)


def _ffn_sublayer(x2, mod, g_pre, g_post, wg, wu, wd, next_f32, seq, weight, tm=1024, tf=512):
    t = x2.shape[0]
    tiles_per_seq = seq // tm
    n_i, n_j = t // tm, D_FF // tf
    in_specs = [
        pl.BlockSpec((tm, D_MODEL), lambda i, j: (i, 0)),
        pl.BlockSpec((1, N_MOD, D_MODEL), lambda i, j: (i // tiles_per_seq, 0, 0)),
        pl.BlockSpec((1, D_MODEL), lambda i, j: (0, 0)),
        pl.BlockSpec((1, D_MODEL), lambda i, j: (0, 0)),
        pl.BlockSpec((D_MODEL, tf), lambda i, j: (0, j)),
        pl.BlockSpec((D_MODEL, tf), lambda i, j: (0, j)),
        pl.BlockSpec((tf, D_MODEL), lambda i, j: (j, 0)),
    ]
    out_specs = [pl.BlockSpec((tm, D_MODEL), lambda i, j: (i, 0))]
    out_shape = [jax.ShapeDtypeStruct((t, D_MODEL), F32)]
    args = [x2, mod, g_pre.reshape(1, D_MODEL), g_post.reshape(1, D_MODEL), wg, wu, wd]
    if next_f32 is not None:
        ng, nu, nd, nl, nw = next_f32
        dr = D_MODEL // n_i
        in_specs += [
            pl.BlockSpec((1, 1, dr, tf), lambda i, j: (nl, nw, i, j)),
            pl.BlockSpec((1, 1, dr, tf), lambda i, j: (nl, nw, i, j)),
            pl.BlockSpec((1, 1, tf, dr), lambda i, j: (nl, nw, j, i)),
        ]
        out_specs += [
            pl.BlockSpec((dr, tf), lambda i, j: (i, j)),
            pl.BlockSpec((dr, tf), lambda i, j: (i, j)),
            pl.BlockSpec((tf, dr), lambda i, j: (j, i)),
        ]
        out_shape += [
            jax.ShapeDtypeStruct((D_MODEL, D_FF), BF16),
            jax.ShapeDtypeStruct((D_MODEL, D_FF), BF16),
            jax.ShapeDtypeStruct((D_FF, D_MODEL), BF16),
        ]
        args += [ng, nu, nd]
    return pl.pallas_call(
        functools.partial(_ffn_kernel, weight=weight, convert_next=next_f32 is not None),
        grid=(n_i, n_j),
        in_specs=in_specs,
        out_specs=out_specs,
        out_shape=out_shape,
        scratch_shapes=[pltpu.VMEM((tm, D_MODEL), BF16)],
        compiler_params=_params(("parallel", "arbitrary"), 62),
        name="ffn_sublayer",
    )(*args)


def _pool_kernel(x_ref, mod_ref, gpre_ref, gpost_ref, w_ref, b_ref, cs_ref, o_ref, hext_ref, *, tm):
    s = pl.program_id(1)

    @pl.when(s == 0)
    def _():
        hext_ref[0:POOL_HALO, :] = jnp.zeros((POOL_HALO, D_MODEL), F32)

    x = x_ref[0]
    hext_ref[POOL_HALO:POOL_HALO + tm, :] = _modulated_norm(x, mod_ref, gpre_ref)
    t_idx = s * tm + lax.broadcasted_iota(jnp.int32, (tm, 1), 0)
    outs = []
    for g, w in enumerate(POOL_WINDOWS):
        cols = pl.ds(g * POOL_GROUP_DIM, POOL_GROUP_DIM)
        hg = hext_ref[pl.ds(POOL_HALO, tm), cols]
        win = hg
        for k in range(1, w):
            win = win + hext_ref[pl.ds(POOL_HALO - k, tm), cols]
        count = jnp.minimum(t_idx + 1, w).astype(F32)
        pooled = win / count - hg
        outs.append(jnp.dot(pooled.astype(BF16), w_ref[g], preferred_element_type=F32) + b_ref[g])
    y = jnp.concatenate(outs, axis=-1) * cs_ref[...]
    o_ref[0] = _gated_residual(x, y, mod_ref, gpost_ref, 1.0)
    hext_ref[0:POOL_HALO, :] = hext_ref[tm:tm + POOL_HALO, :]


def _pool_sublayer(x, mod, g_pre, g_post, w, bias, ch_scale, tm=512):
    b, seq, _ = x.shape
    n_g = len(POOL_WINDOWS)
    return pl.pallas_call(
        functools.partial(_pool_kernel, tm=tm),
        grid=(b, seq // tm),
        in_specs=[
            pl.BlockSpec((1, tm, D_MODEL), lambda i, s: (i, s, 0)),
            pl.BlockSpec((1, N_MOD, D_MODEL), lambda i, s: (i, 0, 0)),
            pl.BlockSpec((1, D_MODEL), lambda i, s: (0, 0)),
            pl.BlockSpec((1, D_MODEL), lambda i, s: (0, 0)),
            pl.BlockSpec((n_g, POOL_GROUP_DIM, POOL_GROUP_DIM), lambda i, s: (0, 0, 0)),
            pl.BlockSpec((n_g, 1, POOL_GROUP_DIM), lambda i, s: (0, 0, 0)),
            pl.BlockSpec((1, D_MODEL), lambda i, s: (0, 0)),
        ],
        out_specs=pl.BlockSpec((1, tm, D_MODEL), lambda i, s: (i, s, 0)),
        out_shape=jax.ShapeDtypeStruct(x.shape, F32),
        scratch_shapes=[pltpu.VMEM((POOL_HALO + tm, D_MODEL), F32)],
        compiler_params=_params(("arbitrary", "arbitrary"), 48),
        name="pool_sublayer",
    )(x, mod, g_pre.reshape(1, D_MODEL), g_post.reshape(1, D_MODEL), w.astype(BF16),
      bias.reshape(n_g, 1, POOL_GROUP_DIM), ch_scale.reshape(1, D_MODEL))


def _mla_proj_kernel(x_ref, mod_ref, gpre_ref, posc_ref, posr_ref, freqr_ref, freqc_ref, sign_ref,
                     wdq_ref, qn_ref, wqt_ref, wdkv_ref, kvn_ref, wuk_ref, wvt_ref,
                     q_ref, k_ref, v_ref):
    h = _modulated_norm(x_ref[0], mod_ref, gpre_ref).astype(BF16)
    contract_last = (((1,), (1,)), ((), ()))
    q_scale = QK_HEAD_DIM ** -0.5 * math.log2(math.e)

    c_q = _rms(jnp.dot(h, wdq_ref[...], preferred_element_type=F32), qn_ref[...]).astype(BF16)
    qt = lax.dot_general(wqt_ref[...], c_q, contract_last, preferred_element_type=F32) * q_scale
    ang_t = freqc_ref[...] * posr_ref[0].astype(F32)
    cos_t = jnp.cos(ang_t)
    sin_t = jnp.sin(ang_t)
    for hd in range(MLA_HEADS):
        base = QK_PAD_DIM * hd
        r0 = base + QK_NOPE_DIM
        x1 = qt[r0:r0 + ROPE_HALF]
        x2 = qt[r0 + ROPE_HALF:r0 + QK_ROPE_DIM]
        q_ref[0, hd, 0:QK_NOPE_DIM, :] = qt[base:r0].astype(BF16)
        q_ref[0, hd, QK_NOPE_DIM:QK_NOPE_DIM + ROPE_HALF, :] = (x1 * cos_t - x2 * sin_t).astype(BF16)
        q_ref[0, hd, QK_NOPE_DIM + ROPE_HALF:QK_HEAD_DIM, :] = (x2 * cos_t + x1 * sin_t).astype(BF16)
        q_ref[0, hd, QK_HEAD_DIM:QK_PAD_DIM, :] = qt[base + QK_HEAD_DIM:base + QK_PAD_DIM].astype(BF16)

    ckv = jnp.dot(h, wdkv_ref[...], preferred_element_type=F32)
    n0 = KV_LORA_RANK
    ang = posc_ref[0].astype(F32) * freqr_ref[...]
    k_rope = (ckv[:, n0:n0 + 128] * jnp.cos(ang) + ckv[:, n0 + 128:n0 + 256] * (jnp.sin(ang) * sign_ref[...]))
    k_rope = k_rope.astype(BF16)
    c_kv = _rms(ckv[:, :n0], kvn_ref[...]).astype(BF16)
    k_nope = jnp.dot(c_kv, wuk_ref[...], preferred_element_type=F32)
    vt = lax.dot_general(wvt_ref[...], c_kv, contract_last, preferred_element_type=F32)
    for hd in range(MLA_HEADS):
        k_ref[0, hd, :, 0:128] = k_nope[:, 128 * hd:128 * (hd + 1)].astype(BF16)
        k_ref[0, hd, :, 128:256] = k_rope
        v_ref[0, hd] = vt[128 * hd:128 * (hd + 1)].astype(BF16)


def _attn_kernel(qt_ref, k_ref, vt_ref, o_ref, m_ref, l_ref, acc_ref, st0_ref, *, tq, heads):
    i = pl.program_id(2)
    neg = jnp.finfo(F32).min
    m_ref[...] = jnp.full_like(m_ref, neg)
    l_ref[...] = jnp.zeros_like(l_ref)
    acc_ref[...] = jnp.zeros_like(acc_ref)

    def scores(g, j):
        start = pl.multiple_of(j * tq, tq)
        return jnp.dot(k_ref[0, g, pl.ds(start, tq), :], qt_ref[0, g], preferred_element_type=F32)

    st0_ref[...] = scores(0, 0)

    def step(j, masked):
        start = pl.multiple_of(j * tq, tq)
        st_next = st0_ref[...]
        for g in range(heads):
            st = st_next
            if g + 1 < heads:
                st_next = scores(g + 1, j)
            elif not masked:
                st0_ref[...] = scores(0, j + 1)
            vt = vt_ref[0, g, :, pl.ds(start, tq)]
            if masked:
                key = lax.broadcasted_iota(jnp.int32, (tq, tq), 0)
                qry = lax.broadcasted_iota(jnp.int32, (tq, tq), 1)
                st = jnp.where(qry >= key, st, neg)
            m_prev = m_ref[g]
            m_new = jnp.maximum(m_prev, jnp.max(st, axis=0, keepdims=True))
            alpha = jnp.exp2(m_prev - m_new)
            p = jnp.exp2(st - m_new)
            l_ref[g] = alpha * l_ref[g] + jnp.sum(p, axis=0, keepdims=True)
            acc_ref[g] = alpha * acc_ref[g] + jnp.dot(vt, p.astype(BF16), preferred_element_type=F32)
            m_ref[g] = m_new

    def body(j, carry):
        step(j, False)
        return carry

    lax.fori_loop(0, i, body, 0)
    step(i, True)
    for g in range(heads):
        o_ref[0, :, V_HEAD_DIM * g:V_HEAD_DIM * (g + 1)] = (acc_ref[g] / l_ref[g]).T.astype(BF16)


def _mla_out_kernel(x_ref, a_ref, mod_ref, gpost_ref, wo_ref, o_ref):
    for r in range(0, x_ref.shape[0], FFN_ROW_CHUNK):
        rows = slice(r, r + FFN_ROW_CHUNK)
        y = jnp.dot(a_ref[rows, :], wo_ref[...], preferred_element_type=F32)
        o_ref[rows, :] = _gated_residual(x_ref[rows, :], y, mod_ref, gpost_ref, 1.0)


def _mla_weights(w_dq, w_uq, w_dkv, w_ukv, w_o):
    uq = w_uq.reshape(Q_LORA_RANK, MLA_HEADS, QK_HEAD_DIM)
    uq = jnp.pad(uq, ((0, 0), (0, 0), (0, QK_PAD_DIM - QK_HEAD_DIM)))
    wqt = uq.reshape(Q_LORA_RANK, MLA_HEADS * QK_PAD_DIM).T
    kr = w_dkv[:, KV_LORA_RANK:]
    kr_sw = jnp.concatenate([kr[:, ROPE_HALF:], kr[:, :ROPE_HALF]], axis=-1)
    zero = jnp.zeros_like(kr)
    wdkv = jnp.concatenate([w_dkv[:, :KV_LORA_RANK], kr, zero, kr_sw, zero], axis=-1)
    ukv = w_ukv.reshape(KV_LORA_RANK, MLA_HEADS, QK_NOPE_DIM + V_HEAD_DIM)
    wuk = ukv[:, :, :QK_NOPE_DIM].reshape(KV_LORA_RANK, -1)
    wvt = ukv[:, :, QK_NOPE_DIM:].reshape(KV_LORA_RANK, -1).T
    return tuple(a.astype(BF16) for a in (w_dq, wqt, wdkv, wuk, wvt, w_o))


def _mla_sublayer(x, mod, g_pre, g_post, positions, w_dq, q_norm, w_uq, w_dkv, kv_norm, w_ukv, w_o,
                  tm=256, tq=512, heads_per_step=4):
    b, seq, _ = x.shape
    wdq, wqt, wdkv, wuk, wvt, wo = _mla_weights(w_dq, w_uq, w_dkv, w_ukv, w_o)
    inv_freq = ROPE_THETA ** (-jnp.arange(0, QK_ROPE_DIM, 2, dtype=F32) / QK_ROPE_DIM)
    freq_row = jnp.tile(inv_freq, 4).reshape(1, 128)
    freq_col = inv_freq.reshape(ROPE_HALF, 1)
    sign = jnp.tile(jnp.concatenate([-jnp.ones((ROPE_HALF,), F32), jnp.ones((ROPE_HALF,), F32)]), 2)
    sign = sign.reshape(1, 128)
    qn = q_norm.reshape(1, -1)
    kvn = kv_norm.reshape(1, -1)
    gpre = g_pre.reshape(1, D_MODEL)
    qt, k, vt = pl.pallas_call(
        _mla_proj_kernel,
        grid=(b, seq // tm),
        in_specs=[
            pl.BlockSpec((1, tm, D_MODEL), lambda i, s: (i, s, 0)),
            pl.BlockSpec((1, N_MOD, D_MODEL), lambda i, s: (i, 0, 0)),
            _resident(gpre),
            pl.BlockSpec((1, tm, 1), lambda i, s: (i, s, 0)),
            pl.BlockSpec((1, 1, tm), lambda i, s: (i, 0, s)),
            _resident(freq_row), _resident(freq_col), _resident(sign), _resident(wdq), _resident(qn),
            _resident(wqt), _resident(wdkv), _resident(kvn), _resident(wuk), _resident(wvt),
        ],
        out_specs=[
            pl.BlockSpec((1, MLA_HEADS, QK_PAD_DIM, tm), lambda i, s: (i, 0, 0, s)),
            pl.BlockSpec((1, MLA_HEADS, tm, QK_PAD_DIM), lambda i, s: (i, 0, s, 0)),
            pl.BlockSpec((1, MLA_HEADS, V_HEAD_DIM, tm), lambda i, s: (i, 0, 0, s)),
        ],
        out_shape=[
            jax.ShapeDtypeStruct((b, MLA_HEADS, QK_PAD_DIM, seq), BF16),
            jax.ShapeDtypeStruct((b, MLA_HEADS, seq, QK_PAD_DIM), BF16),
            jax.ShapeDtypeStruct((b, MLA_HEADS, V_HEAD_DIM, seq), BF16),
        ],
        compiler_params=_params(("parallel", "parallel"), 56),
        name="mla_proj",
    )(x, mod, gpre, positions.reshape(b, seq, 1), positions.reshape(b, 1, seq), freq_row, freq_col, sign,
      wdq, qn, wqt, wdkv, kvn, wuk, wvt)

    hg = heads_per_step
    attn = pl.pallas_call(
        functools.partial(_attn_kernel, tq=tq, heads=hg),
        grid=(b, MLA_HEADS // hg, seq // tq),
        in_specs=[
            pl.BlockSpec((1, hg, QK_PAD_DIM, tq), lambda i, hd, s: (i, hd, 0, s)),
            pl.BlockSpec((1, hg, seq, QK_PAD_DIM), lambda i, hd, s: (i, hd, 0, 0)),
            pl.BlockSpec((1, hg, V_HEAD_DIM, seq), lambda i, hd, s: (i, hd, 0, 0)),
        ],
        out_specs=pl.BlockSpec((1, tq, hg * V_HEAD_DIM), lambda i, hd, s: (i, s, hd)),
        out_shape=jax.ShapeDtypeStruct((b, seq, MLA_HEADS * V_HEAD_DIM), BF16),
        scratch_shapes=[pltpu.VMEM((hg, 1, tq), F32), pltpu.VMEM((hg, 1, tq), F32),
                        pltpu.VMEM((hg, V_HEAD_DIM, tq), F32), pltpu.VMEM((tq, tq), F32)],
        compiler_params=_params(("parallel", "parallel", "arbitrary"), 48),
        name="mla_attention",
    )(qt, k, vt)

    t = b * seq
    tmo = 512
    tiles_per_seq = seq // tmo
    out = pl.pallas_call(
        _mla_out_kernel,
        grid=(t // tmo,),
        in_specs=[
            pl.BlockSpec((tmo, D_MODEL), lambda i: (i, 0)),
            pl.BlockSpec((tmo, MLA_HEADS * V_HEAD_DIM), lambda i: (i, 0)),
            pl.BlockSpec((1, N_MOD, D_MODEL), lambda i: (i // tiles_per_seq, 0, 0)),
            pl.BlockSpec((1, D_MODEL), lambda i: (0, 0)),
            _resident(wo),
        ],
        out_specs=pl.BlockSpec((tmo, D_MODEL), lambda i: (i, 0)),
        out_shape=jax.ShapeDtypeStruct((t, D_MODEL), F32),
        compiler_params=_params(("parallel",), 48),
        name="mla_out",
    )(x.reshape(t, D_MODEL), attn.reshape(t, -1), mod, g_post.reshape(1, D_MODEL), wo)
    return out.reshape(b, seq, D_MODEL)


def kernel(x, c, positions, ada_w, ada_b, norm_pre, norm_post, ffn_w_gate, ffn_w_up, ffn_w_down, pool_w, pool_b, pool_scale, mla_w_dq, mla_q_norm, mla_w_uq, mla_w_dkv, mla_kv_norm, mla_w_ukv, mla_w_o):
    b, seq, _ = x.shape
    mod = _modulation(c, ada_w, ada_b)
    ffn_w = tuple(w[0, 0].astype(BF16) for w in (ffn_w_gate, ffn_w_up, ffn_w_down))

    def ffn(x, layer, sub, which):
        nonlocal ffn_w
        nxt = 2 * layer + which + 1
        next_f32 = (ffn_w_gate, ffn_w_up, ffn_w_down, nxt // 2, nxt % 2) if nxt < 2 * DEPTH else None
        outs = _ffn_sublayer(x.reshape(b * seq, D_MODEL), mod[layer, :, sub], norm_pre[layer, sub],
                             norm_post[layer, sub], *ffn_w, next_f32, seq, 0.5)
        ffn_w = tuple(outs[1:])
        return outs[0].reshape(b, seq, D_MODEL)

    for layer in range(DEPTH):
        x = ffn(x, layer, 0, 0)
        j = layer // 2
        if layer % 2 == 0:
            x = _pool_sublayer(x, mod[layer, :, 1], norm_pre[layer, 1], norm_post[layer, 1],
                               pool_w[j], pool_b[j], pool_scale[j])
        else:
            x = _mla_sublayer(x, mod[layer, :, 1], norm_pre[layer, 1], norm_post[layer, 1], positions,
                              mla_w_dq[j], mla_q_norm[j], mla_w_uq[j], mla_w_dkv[j], mla_kv_norm[j],
                              mla_w_ukv[j], mla_w_o[j])
        x = ffn(x, layer, 2, 1)
    return x
```
